```python
import math
import jax, jax.numpy as jnp
from jax import lax
import numpy as np

D_MODEL = 4096
BATCH = 4
SEQ = 2048
DEPTH = 1
DEC_BATCH = 2
DEC_SEQ = 8192
PAST_LEN = 128

A_HEADS = 8
A_HEAD_DIM = 128
A_V_DIM = 2 * A_HEAD_DIM
A_WIDTH = A_HEADS * A_V_DIM
ROT_DIM = A_HEAD_DIM // 4
ROPE_THETA = 500000.0
Q_BLOCK = 128
B_GROUPS = 8
B_WIDTH = 2048
B_GROUP_DIM = B_WIDTH // B_GROUPS
CHUNK = 128
QK_COLS = A_HEADS * 2 * A_HEAD_DIM
COL_Q = 0
COL_K = COL_Q + QK_COLS
COL_V = COL_K + QK_COLS
COL_U = COL_V + A_WIDTH
COL_VG = COL_U + B_WIDTH
COL_GATE = COL_VG + B_WIDTH
IN_COLS = COL_GATE + 2 * D_MODEL
N_MEM = 256
C_HEADS = 4
C_HEAD_DIM = D_MODEL // C_HEADS
PEER_HEADS = 8
N_KEYS = 128
N_EXPERTS = N_KEYS * N_KEYS
PEER_TOPK = 16
D_QUERY = 256
D_HALF = D_QUERY // 2
PEER_BLOCK = 128
EPS = 1e-6

kernel_name = 'hybrid_diffattn_sgu_peer_encoder'


def rmsnorm(x, g):
    xf = x.astype(jnp.float32)
    y = xf * lax.rsqrt(jnp.mean(xf * xf, axis=-1, keepdims=True) + EPS)
    return (y * g.astype(jnp.float32)).astype(x.dtype)


def layernorm(x, g, b):
    xf = x.astype(jnp.float32)
    mu = jnp.mean(xf, axis=-1, keepdims=True)
    xc = xf - mu
    y = xc * lax.rsqrt(jnp.mean(xc * xc, axis=-1, keepdims=True) + EPS)
    return (y * g.astype(jnp.float32) + b.astype(jnp.float32)).astype(x.dtype)


def partial_rope(x):
    S = x.shape[1]
    half = ROT_DIM // 2
    freqs = ROPE_THETA ** (-jnp.arange(half, dtype=jnp.float32) * 2.0 / ROT_DIM)
    ang = jnp.arange(S, dtype=jnp.float32)[:, None] * freqs[None, :]
    cos = jnp.cos(ang)[None, :, None, None, :]
    sin = jnp.sin(ang)[None, :, None, None, :]
    xf = x.astype(jnp.float32)
    x1 = xf[..., :half]
    x2 = xf[..., half:ROT_DIM]
    out = jnp.concatenate([x1 * cos - x2 * sin, x2 * cos + x1 * sin, xf[..., ROT_DIM:]], axis=-1)
    return out.astype(x.dtype)


def diff_attention(q, k, v, lam):
    B, S = q.shape[:2]
    nb = S // Q_BLOCK
    scale = A_HEAD_DIM ** -0.5
    qb = q.reshape(B, nb, Q_BLOCK, A_HEADS, 2, A_HEAD_DIM).transpose(1, 0, 2, 3, 4, 5)

    def block(qi):
        s = jnp.einsum('bqhjd,bkhjd->bhjqk', qi, k).astype(jnp.float32) * scale
        p = jax.nn.softmax(s, axis=-1)
        a = p[:, :, 0] - lam * p[:, :, 1]
        return jnp.einsum('bhqk,bkhe->bqhe', a.astype(v.dtype), v)

    o = lax.map(block, qb)
    return o.transpose(1, 0, 2, 3, 4).reshape(B, S, A_HEADS, A_V_DIM)


def spatial_gating(u, vg, ln_g, ln_b, w_s, b_s):
    B, S = u.shape[:2]
    nc = S // CHUNK
    vn = layernorm(vg, ln_g, ln_b).reshape(B, nc, CHUNK, B_GROUPS, B_GROUP_DIM)
    mixed = jnp.einsum('gpq,bnqgc->bnpgc', w_s, vn) + b_s.T[None, None, :, :, None]
    return u * mixed.reshape(B, S, B_WIDTH)


def memory_cross_attention(h, mem, w_cq, w_ck, w_cv, w_co):
    B, S = h.shape[:2]
    M = mem.shape[1]
    q = (h @ w_cq).reshape(B, S, C_HEADS, C_HEAD_DIM)
    k = (mem @ w_ck).reshape(B, M, C_HEADS, C_HEAD_DIM)
    v = (mem @ w_cv).reshape(B, M, C_HEADS, C_HEAD_DIM)
    s = jnp.einsum('bqhd,bkhd->bhqk', q, k).astype(jnp.float32) * (C_HEAD_DIM ** -0.5)
    p = jax.nn.softmax(s, axis=-1)
    o = jnp.einsum('bhqk,bkhd->bqhd', p.astype(v.dtype), v).reshape(B, S, D_MODEL)
    return o @ w_co


def peer(h, w_pq, sub_keys, u_tab, v_tab):
    B, S, D = h.shape
    T = B * S
    xf = h.reshape(T, D)
    q = (xf @ w_pq).reshape(T, PEER_HEADS, 2, D_HALF)
    s = jnp.einsum('thcd,hcnd->thcn', q, sub_keys).astype(jnp.float32)
    sv, si = lax.top_k(s, PEER_TOPK)
    cand_s = (sv[:, :, 0, :, None] + sv[:, :, 1, None, :]).reshape(T, PEER_HEADS, PEER_TOPK * PEER_TOPK)
    cand_i = (si[:, :, 0, :, None] * N_KEYS + si[:, :, 1, None, :]).reshape(T, PEER_HEADS, PEER_TOPK * PEER_TOPK)
    top_s, top_j = lax.top_k(cand_s, PEER_TOPK)
    ids = jnp.take_along_axis(cand_i, top_j, axis=-1)
    gates = jax.nn.softmax(top_s, axis=-1).astype(h.dtype)
    nb = T // PEER_BLOCK

    def block(args):
        xb, idb, gb = args
        a = jnp.einsum('td,thkd->thk', xb, u_tab[idb])
        w = gb * jax.nn.gelu(a, approximate=False)
        return jnp.einsum('thk,thkd->td', w, v_tab[idb])

    out = lax.map(block, (xf.reshape(nb, PEER_BLOCK, D),
                          ids.reshape(nb, PEER_BLOCK, PEER_HEADS, PEER_TOPK),
                          gates.reshape(nb, PEER_BLOCK, PEER_HEADS, PEER_TOPK)))
    return out.reshape(B, S, D)


def encoder(x, mem, norm_mix_g, w_in, lambda_q1, lambda_k1, lambda_q2, lambda_k2, subln_g,
            sgu_ln_g, sgu_ln_b, w_spatial, b_spatial, w_branch_a, w_branch_b, w_out,
            norm_cross_g, norm_mem_g, w_cq, w_ck, w_cv, w_co,
            norm_ffn_g, w_peer_q, peer_keys, expert_u, expert_v, final_norm_g):
    B, S = x.shape[:2]
    for l in range(DEPTH):
        xn = rmsnorm(x, norm_mix_g[l])
        z = xn @ w_in[l]
        q = partial_rope(z[..., COL_Q:COL_K].reshape(B, S, A_HEADS, 2, A_HEAD_DIM))
        k = partial_rope(z[..., COL_K:COL_V].reshape(B, S, A_HEADS, 2, A_HEAD_DIM))
        v = z[..., COL_V:COL_U].reshape(B, S, A_HEADS, A_V_DIM)
        lam_init = 0.8 - 0.6 * math.exp(-0.3 * l)
        lam = (jnp.exp(jnp.sum(lambda_q1[l].astype(jnp.float32) * lambda_k1[l].astype(jnp.float32)))
               - jnp.exp(jnp.sum(lambda_q2[l].astype(jnp.float32) * lambda_k2[l].astype(jnp.float32)))
               + lam_init)
        o_a = rmsnorm(diff_attention(q, k, v, lam), subln_g[l]) * (1.0 - lam_init)
        y_a = o_a.reshape(B, S, A_WIDTH) @ w_branch_a[l]
        u_b = jax.nn.gelu(z[..., COL_U:COL_VG], approximate=False)
        v_b = jax.nn.gelu(z[..., COL_VG:COL_GATE], approximate=False)
        y_b = spatial_gating(u_b, v_b, sgu_ln_g[l], sgu_ln_b[l], w_spatial[l], b_spatial[l]) @ w_branch_b[l]
        gts = jax.nn.sigmoid(z[..., COL_GATE:])
        merged = gts[..., :D_MODEL] * y_a + gts[..., D_MODEL:] * y_b
        x = x + merged @ w_out[l]
        x = x + memory_cross_attention(rmsnorm(x, norm_cross_g[l]), rmsnorm(mem, norm_mem_g[l]),
                                       w_cq[l], w_ck[l], w_cv[l], w_co[l])
        x = x + peer(rmsnorm(x, norm_ffn_g[l]), w_peer_q[l], peer_keys[l], expert_u[l], expert_v[l])
    return rmsnorm(x, final_norm_g)


def setup_inputs(seed: int = 0) -> dict:
    key = jax.random.key(seed)
    ks = jax.random.split(key, 32)
    f32 = jnp.float32

    def nrm(k, shape, scale):
        return jax.random.normal(k, shape, dtype=f32) * scale

    def gain(k, shape):
        return 1.0 + 0.02 * jax.random.normal(k, shape, dtype=f32)

    L, D = DEPTH, D_MODEL
    return {
        'x_prompt': nrm(ks[0], (BATCH, SEQ, D), 1.0),
        'x_sample': nrm(ks[1], (DEC_BATCH, DEC_SEQ, D), 1.0),
        'mem_prompt': nrm(ks[2], (BATCH, N_MEM, D), 1.0),
        'mem_sample': nrm(ks[3], (DEC_BATCH, N_MEM, D), 1.0),
        'norm_mix_g': gain(ks[4], (L, D)),
        'w_in': nrm(ks[5], (L, D, IN_COLS), D ** -0.5),
        'lambda_q1': nrm(ks[6], (L, A_HEAD_DIM), 0.1),
        'lambda_k1': nrm(ks[7], (L, A_HEAD_DIM), 0.1),
        'lambda_q2': nrm(ks[8], (L, A_HEAD_DIM), 0.1),
        'lambda_k2': nrm(ks[9], (L, A_HEAD_DIM), 0.1),
        'subln_g': gain(ks[10], (L, A_V_DIM)),
        'sgu_ln_g': gain(ks[11], (L, B_WIDTH)),
        'sgu_ln_b': nrm(ks[12], (L, B_WIDTH), 0.02),
        'w_spatial': nrm(ks[13], (L, B_GROUPS, CHUNK, CHUNK), CHUNK ** -0.5),
        'b_spatial': gain(ks[14], (L, B_GROUPS, CHUNK)),
        'w_branch_a': nrm(ks[15], (L, A_WIDTH, D), A_WIDTH ** -0.5),
        'w_branch_b': nrm(ks[16], (L, B_WIDTH, D), B_WIDTH ** -0.5),
        'w_out': nrm(ks[17], (L, D, D), D ** -0.5),
        'norm_cross_g': gain(ks[18], (L, D)),
        'norm_mem_g': gain(ks[19], (L, D)),
        'w_cq': nrm(ks[20], (L, D, D), D ** -0.5),
        'w_ck': nrm(ks[21], (L, D, D), D ** -0.5),
        'w_cv': nrm(ks[22], (L, D, D), D ** -0.5),
        'w_co': nrm(ks[23], (L, D, D), D ** -0.5),
        'norm_ffn_g': gain(ks[24], (L, D)),
        'w_peer_q': nrm(ks[25], (L, D, PEER_HEADS * D_QUERY), D ** -0.5),
        'peer_keys': nrm(ks[26], (L, PEER_HEADS, 2, N_KEYS, D_HALF), D_HALF ** -0.5),
        'expert_u': nrm(ks[27], (L, N_EXPERTS, D), D ** -0.5),
        'expert_v': nrm(ks[28], (L, N_EXPERTS, D), PEER_HEADS ** -0.5),
        'final_norm_g': gain(ks[29], (D,)),
    }


def reference(x_prompt, x_sample, mem_prompt, mem_sample, norm_mix_g, w_in, lambda_q1, lambda_k1,
              lambda_q2, lambda_k2, subln_g, sgu_ln_g, sgu_ln_b, w_spatial, b_spatial, w_branch_a,
              w_branch_b, w_out, norm_cross_g, norm_mem_g, w_cq, w_ck, w_cv, w_co, norm_ffn_g,
              w_peer_q, peer_keys, expert_u, expert_v, final_norm_g):
    weights = (norm_mix_g, w_in, lambda_q1, lambda_k1, lambda_q2, lambda_k2, subln_g,
               sgu_ln_g, sgu_ln_b, w_spatial, b_spatial, w_branch_a, w_branch_b, w_out,
               norm_cross_g, norm_mem_g, w_cq, w_ck, w_cv, w_co,
               norm_ffn_g, w_peer_q, peer_keys, expert_u, expert_v, final_norm_g)
    y_prompt = encoder(x_prompt, mem_prompt, *weights)
    y_sample = encoder(x_sample, mem_sample, *weights)
    return (y_prompt, y_sample)
```

```python
import functools
import math

import jax
import jax.numpy as jnp
from jax import lax
from jax.experimental import pallas as pl
from jax.experimental.pallas import tpu as pltpu

F32 = jnp.float32
BF16 = jnp.bfloat16

D_MODEL = 4096
A_HEADS = 8
A_HEAD_DIM = 128
A_V_DIM = 2 * A_HEAD_DIM
A_WIDTH = A_HEADS * A_V_DIM
ROT_DIM = A_HEAD_DIM // 4
ROPE_THETA = 500000.0
B_GROUPS = 8
B_WIDTH = 2048
B_GROUP_DIM = B_WIDTH // B_GROUPS
CHUNK = 128
QK_COLS = A_HEADS * 2 * A_HEAD_DIM
COL_Q = 0
COL_K = COL_Q + QK_COLS
COL_V = COL_K + QK_COLS
COL_U = COL_V + A_WIDTH
COL_VG = COL_U + B_WIDTH
COL_GATE = COL_VG + B_WIDTH
IN_COLS = COL_GATE + 2 * D_MODEL
N_MEM = 256
C_HEADS = 4
C_HEAD_DIM = D_MODEL // C_HEADS
PEER_HEADS = 8
N_KEYS = 128
N_EXPERTS = N_KEYS * N_KEYS
PEER_TOPK = 16
D_QUERY = 256
D_HALF = D_QUERY // 2
EPS = 1e-6
LAM_INIT = 0.8 - 0.6 * math.exp(-0.3 * 0)

VMEM_LIMIT_BYTES = 58 * 1024 * 1024
LANES = 128

_NT = (((1,), (1,)), ((), ()))


def _params(*sem):
    return pltpu.CompilerParams(dimension_semantics=sem, vmem_limit_bytes=VMEM_LIMIT_BYTES)


def _gelu(x):
    return 0.5 * x * (1.0 + lax.erf(x * (2.0 ** -0.5)))


def _rms_body(x_ref, g_ref, o_ref):
    x = x_ref[...].astype(F32)
    ms = jnp.mean(x * x, axis=-1, keepdims=True)
    o_ref[...] = (x * lax.rsqrt(ms + EPS) * g_ref[...]).astype(o_ref.dtype)


def _rmsnorm(x, g, out_dtype, tm=512):
    m, d = x.shape
    tm = min(tm, m)
    return pl.pallas_call(
        _rms_body,
        grid=(m // tm,),
        in_specs=[pl.BlockSpec((tm, d), lambda i: (i, 0)), pl.BlockSpec((1, d), lambda i: (0, 0))],
        out_specs=pl.BlockSpec((tm, d), lambda i: (i, 0)),
        out_shape=jax.ShapeDtypeStruct((m, d), out_dtype),
        compiler_params=_params("parallel"),
        name="rmsnorm",
    )(x, g.reshape(1, d).astype(F32))


def _add_rms_body(x_ref, y_ref, g_ref, o_ref):
    x = x_ref[...] + y_ref[...].astype(F32)
    ms = jnp.mean(x * x, axis=-1, keepdims=True)
    o_ref[...] = x * lax.rsqrt(ms + EPS) * g_ref[...]


def _add_rmsnorm(x, y, g, tm=512):
    m, d = x.shape
    tm = min(tm, m)
    return pl.pallas_call(
        _add_rms_body,
        grid=(m // tm,),
        in_specs=[pl.BlockSpec((tm, d), lambda i: (i, 0)), pl.BlockSpec((tm, d), lambda i: (i, 0)),
                  pl.BlockSpec((1, d), lambda i: (0, 0))],
        out_specs=pl.BlockSpec((tm, d), lambda i: (i, 0)),
        out_shape=jax.ShapeDtypeStruct((m, d), F32),
        compiler_params=_params("parallel"),
        name="add_rmsnorm",
    )(x, y, g.reshape(1, d).astype(F32))


def _win_body(x_ref, w_ref, c_ref, sa_ref, sb_ref, o_ref, *, tn):
    j = pl.program_id(1)
    acc = jnp.dot(x_ref[...], w_ref[...], preferred_element_type=F32)
    n_q = QK_COLS // tn
    j_v = COL_V // tn
    j_u = COL_U // tn
    j_g = COL_GATE // tn

    @pl.when(j < j_v)
    def _():
        scale = jnp.where(j < n_q, A_HEAD_DIM ** -0.5, 1.0).astype(F32)
        c = c_ref[...]
        sa = sa_ref[...]
        sb = sb_ref[...]
        for g in range(tn // LANES):
            x = acc[:, g * LANES:(g + 1) * LANES]
            y = x * c + pltpu.roll(x, LANES - ROT_DIM // 2, 1) * sa + pltpu.roll(x, ROT_DIM // 2, 1) * sb
            o_ref[:, g * LANES:(g + 1) * LANES] = (y * scale).astype(o_ref.dtype)

    @pl.when((j >= j_v) & (j < j_u))
    def _():
        o_ref[...] = acc.astype(o_ref.dtype)

    @pl.when((j >= j_u) & (j < j_g))
    def _():
        o_ref[...] = _gelu(acc).astype(o_ref.dtype)

    @pl.when(j >= j_g)
    def _():
        o_ref[...] = (1.0 / (1.0 + jnp.exp(-acc))).astype(o_ref.dtype)


def _in_proj(xn, w_in, rope, seq, tm=1024, tn=512):
    m, k = xn.shape
    n = w_in.shape[1]
    tm = min(tm, seq)
    nsb = seq // tm
    tab = pl.BlockSpec((tm, LANES), lambda i, j: (i % nsb, 0))
    return pl.pallas_call(
        functools.partial(_win_body, tn=tn),
        grid=(m // tm, n // tn),
        in_specs=[pl.BlockSpec((tm, k), lambda i, j: (i, 0)), pl.BlockSpec((k, tn), lambda i, j: (0, j)),
                  tab, tab, tab],
        out_specs=pl.BlockSpec((tm, tn), lambda i, j: (i, j)),
        out_shape=jax.ShapeDtypeStruct((m, n), BF16),
        compiler_params=_params("parallel", "arbitrary"),
        name="in_proj",
    )(xn, w_in, *rope)


def _rope_tables(seq):
    half = ROT_DIM // 2
    freqs = ROPE_THETA ** (-jnp.arange(half, dtype=F32) * 2.0 / ROT_DIM)
    ang = jnp.arange(seq, dtype=F32)[:, None] * freqs[None, :]
    cos, sin = jnp.cos(ang), jnp.sin(ang)
    ones = jnp.ones((seq, LANES - ROT_DIM), F32)
    zeros = jnp.zeros((seq, LANES - half), F32)
    c = jnp.concatenate([cos, cos, ones], axis=1)
    sa = jnp.concatenate([-sin, zeros], axis=1)
    sb = jnp.concatenate([jnp.zeros((seq, half), F32), sin, jnp.zeros((seq, LANES - ROT_DIM), F32)], axis=1)
    return c, sa, sb


def _dattn_body(q_ref, k_ref, v_ref, lq1_ref, lk1_ref, lq2_ref, lk2_ref, g_ref, o_ref):
    q = q_ref[...]
    k = k_ref[...]
    lam = (jnp.exp(jnp.sum(lq1_ref[...] * lk1_ref[...], axis=-1, keepdims=True))
           - jnp.exp(jnp.sum(lq2_ref[...] * lk2_ref[...], axis=-1, keepdims=True)) + LAM_INIT)

    def softmax_parts(qc, kc):
        s = lax.dot_general(qc, kc, _NT, preferred_element_type=F32)
        e = jnp.exp(s - jnp.max(s, axis=-1, keepdims=True))
        return e, jnp.sum(e, axis=-1, keepdims=True)

    e1, l1 = softmax_parts(q[:, :A_HEAD_DIM], k[:, :A_HEAD_DIM])
    e2, l2 = softmax_parts(q[:, A_HEAD_DIM:], k[:, A_HEAD_DIM:])
    a = e1 * (1.0 / l1) - e2 * (lam / l2)
    o = jnp.dot(a.astype(BF16), v_ref[...], preferred_element_type=F32)
    ms = jnp.mean(o * o, axis=-1, keepdims=True)
    o_ref[...] = (o * lax.rsqrt(ms + EPS) * g_ref[...] * (1.0 - LAM_INIT)).astype(o_ref.dtype)


def _diff_attention(z, lam_vecs, subln_g, batch, seq, tq=128):
    tq = min(tq, seq)
    nq = seq // tq
    vec = pl.BlockSpec((1, A_HEAD_DIM), lambda b, h, i: (0, 0))
    return pl.pallas_call(
        _dattn_body,
        grid=(batch, A_HEADS, nq),
        in_specs=[
            pl.BlockSpec((tq, A_V_DIM), lambda b, h, i: (b * nq + i, COL_Q // A_V_DIM + h)),
            pl.BlockSpec((seq, A_V_DIM), lambda b, h, i: (b, COL_K // A_V_DIM + h)),
            pl.BlockSpec((seq, A_V_DIM), lambda b, h, i: (b, COL_V // A_V_DIM + h)),
            vec, vec, vec, vec,
            pl.BlockSpec((1, A_V_DIM), lambda b, h, i: (0, 0)),
        ],
        out_specs=pl.BlockSpec((tq, A_V_DIM), lambda b, h, i: (b * nq + i, h)),
        out_shape=jax.ShapeDtypeStruct((batch * seq, A_WIDTH), BF16),
        compiler_params=_params("parallel", "parallel", "arbitrary"),
        name="diff_attention",
    )(z, z, z, *lam_vecs, subln_g)


def _sgu_body(u_ref, vg_ref, lg_ref, lb_ref, ws_ref, bs_ref, o_ref, *, nchunk):
    vg = vg_ref[...].astype(F32)
    mu = jnp.mean(vg, axis=-1, keepdims=True)
    xc = vg - mu
    var = jnp.mean(xc * xc, axis=-1, keepdims=True)
    vn = (xc * lax.rsqrt(var + EPS) * lg_ref[...] + lb_ref[...]).astype(BF16)
    for c in range(nchunk):
        rows = slice(c * CHUNK, (c + 1) * CHUNK)
        for g in range(B_GROUPS):
            cols = slice(g * B_GROUP_DIM, (g + 1) * B_GROUP_DIM)
            mixed = jnp.dot(ws_ref[g], vn[rows, cols], preferred_element_type=F32) + bs_ref[g]
            o_ref[rows, cols] = (u_ref[rows, cols].astype(F32) * mixed).astype(o_ref.dtype)


def _sgu(z, ln_g, ln_b, w_s, b_s, tr=512):
    m = z.shape[0]
    tr = min(tr, m)
    return pl.pallas_call(
        functools.partial(_sgu_body, nchunk=tr // CHUNK),
        grid=(m // tr,),
        in_specs=[
            pl.BlockSpec((tr, B_WIDTH), lambda i: (i, COL_U // B_WIDTH)),
            pl.BlockSpec((tr, B_WIDTH), lambda i: (i, COL_VG // B_WIDTH)),
            pl.BlockSpec((1, B_WIDTH), lambda i: (0, 0)),
            pl.BlockSpec((1, B_WIDTH), lambda i: (0, 0)),
            pl.BlockSpec((B_GROUPS, CHUNK, CHUNK), lambda i: (0, 0, 0)),
            pl.BlockSpec((B_GROUPS, CHUNK, 1), lambda i: (0, 0, 0)),
        ],
        out_specs=pl.BlockSpec((tr, B_WIDTH), lambda i: (i, 0)),
        out_shape=jax.ShapeDtypeStruct((m, B_WIDTH), BF16),
        compiler_params=_params("parallel"),
        name="sgu",
    )(z, z, ln_g, ln_b, w_s, b_s)


def _branch_body(oa_ref, sg_ref, wa_ref, wb_ref, ga_ref, gb_ref, o_ref):
    ya = jnp.dot(oa_ref[...], wa_ref[...], preferred_element_type=F32)
    yb = jnp.dot(sg_ref[...], wb_ref[...], preferred_element_type=F32)
    o_ref[...] = (ga_ref[...].astype(F32) * ya + gb_ref[...].astype(F32) * yb).astype(o_ref.dtype)


def _branch_merge(o_a, sgu, w_a, w_b, z, tm=512, tn=1024):
    m = o_a.shape[0]
    tm = min(tm, m)
    return pl.pallas_call(
        _branch_body,
        grid=(m // tm, D_MODEL // tn),
        in_specs=[
            pl.BlockSpec((tm, A_WIDTH), lambda i, j: (i, 0)),
            pl.BlockSpec((tm, B_WIDTH), lambda i, j: (i, 0)),
            pl.BlockSpec((A_WIDTH, tn), lambda i, j: (0, j)),
            pl.BlockSpec((B_WIDTH, tn), lambda i, j: (0, j)),
            pl.BlockSpec((tm, tn), lambda i, j: (i, COL_GATE // tn + j)),
            pl.BlockSpec((tm, tn), lambda i, j: (i, (COL_GATE + D_MODEL) // tn + j)),
        ],
        out_specs=pl.BlockSpec((tm, tn), lambda i, j: (i, j)),
        out_shape=jax.ShapeDtypeStruct((m, D_MODEL), BF16),
        compiler_params=_params("parallel", "arbitrary"),
        name="branch_merge",
    )(o_a, sgu, w_a, w_b, z, z)


def _mm_body(a_ref, w_ref, o_ref):
    o_ref[...] = jnp.dot(a_ref[...], w_ref[...], preferred_element_type=F32).astype(o_ref.dtype)


def _mm_res_body(a_ref, w_ref, r_ref, o_ref):
    o_ref[...] = r_ref[...] + jnp.dot(a_ref[...], w_ref[...], preferred_element_type=F32)


def _matmul(a, w, residual=None, out_dtype=BF16, tm=512, tn=1024):
    m, k = a.shape
    n = w.shape[1]
    tm = min(tm, m)
    tn = min(tn, n)
    in_specs = [pl.BlockSpec((tm, k), lambda i, j: (i, 0)), pl.BlockSpec((k, tn), lambda i, j: (0, j))]
    args = [a, w]
    body = _mm_body
    if residual is not None:
        in_specs.append(pl.BlockSpec((tm, tn), lambda i, j: (i, j)))
        args.append(residual)
        body = _mm_res_body
        out_dtype = F32
    return pl.pallas_call(
        body,
        grid=(m // tm, n // tn),
        in_specs=in_specs,
        out_specs=pl.BlockSpec((tm, tn), lambda i, j: (i, j)),
        out_shape=jax.ShapeDtypeStruct((m, n), out_dtype),
        compiler_params=_params("parallel", "arbitrary"),
        name="matmul_res" if residual is not None else "matmul",
    )(*args)


def _mm_nt_body(a_ref, b_ref, o_ref):
    o_ref[...] = lax.dot_general(a_ref[...], b_ref[...], _NT, preferred_element_type=F32).astype(o_ref.dtype)


def _matmul_nt(a, b, out_dtype=F32, tm=512, tn=1024):
    m, k = a.shape
    n = b.shape[0]
    tm = min(tm, m)
    tn = min(tn, n)
    return pl.pallas_call(
        _mm_nt_body,
        grid=(m // tm, n // tn),
        in_specs=[pl.BlockSpec((tm, k), lambda i, j: (i, 0)), pl.BlockSpec((tn, k), lambda i, j: (j, 0))],
        out_specs=pl.BlockSpec((tm, tn), lambda i, j: (i, j)),
        out_shape=jax.ShapeDtypeStruct((m, n), out_dtype),
        compiler_params=_params("parallel", "arbitrary"),
        name="matmul_nt",
    )(a, b)


def _cross_body(h_ref, wq_ref, k_ref, v_ref, o_ref):
    q = jnp.dot(h_ref[...], wq_ref[...], preferred_element_type=F32) * (C_HEAD_DIM ** -0.5)
    s = lax.dot_general(q.astype(BF16), k_ref[...], _NT, preferred_element_type=F32)
    e = jnp.exp(s - jnp.max(s, axis=-1, keepdims=True))
    p = e * (1.0 / jnp.sum(e, axis=-1, keepdims=True))
    o_ref[...] = jnp.dot(p.astype(BF16), v_ref[...], preferred_element_type=F32).astype(o_ref.dtype)


def _cross_attention(h, w_cq, kc, vc, batch, seq, tm=512):
    tm = min(tm, seq)
    ns = seq // tm
    return pl.pallas_call(
        _cross_body,
        grid=(batch, ns, C_HEADS),
        in_specs=[
            pl.BlockSpec((tm, D_MODEL), lambda b, i, hd: (b * ns + i, 0)),
            pl.BlockSpec((D_MODEL, C_HEAD_DIM), lambda b, i, hd: (0, hd)),
            pl.BlockSpec((N_MEM, C_HEAD_DIM), lambda b, i, hd: (b, hd)),
            pl.BlockSpec((N_MEM, C_HEAD_DIM), lambda b, i, hd: (b, hd)),
        ],
        out_specs=pl.BlockSpec((tm, C_HEAD_DIM), lambda b, i, hd: (b * ns + i, hd)),
        out_shape=jax.ShapeDtypeStruct((batch * seq, D_MODEL), BF16),
        compiler_params=_params("parallel", "parallel", "arbitrary"),
        name="cross_attention",
    )(h, w_cq, kc, vc)


def _top16_rows(s, vals_ref):
    n = s.shape[0]
    rows = lax.broadcasted_iota(jnp.int32, s.shape, 0)
    for r in range(PEER_TOPK):
        m = jnp.max(s, axis=0, keepdims=True)
        vals_ref[r:r + 1, :] = m
        if r + 1 < PEER_TOPK:
            first = jnp.min(jnp.where(s == m, rows, n), axis=0, keepdims=True)
            s = jnp.where(rows == first, -jnp.inf, s)


def _route_body(q_ref, keys_ref, s1_ref, s2_ref, e1_ref, e2_ref, tau_ref, sv_ref, cand_ref, top_ref):
    for h in range(PEER_HEADS):
        s = []
        for c in range(2):
            hc = 2 * h + c
            q = q_ref[hc * D_HALF:(hc + 1) * D_HALF, :].astype(BF16)
            sc = jnp.dot(keys_ref[hc], q, preferred_element_type=F32)
            s.append(sc)
            _top16_rows(sc, sv_ref.at[c])
        sv2 = sv_ref[1]
        for a in range(PEER_TOPK):
            cand_ref[a * PEER_TOPK:(a + 1) * PEER_TOPK, :] = sv_ref[0, a:a + 1, :] + sv2
        _top16_rows(cand_ref[...], top_ref)
        top = top_ref[...]
        z = jnp.sum(jnp.exp(top - top[0:1, :]), axis=0, keepdims=True)
        s1_ref[h] = s[0]
        s2_ref[h] = s[1]
        e1_ref[h] = jnp.exp(s[0] - sv_ref[0, 0:1, :]) * (1.0 / z)
        e2_ref[h] = jnp.exp(s[1] - sv_ref[1, 0:1, :])
        tau_ref[h:h + 1, :] = top[PEER_TOPK - 1:PEER_TOPK, :]


def _peer_route(q_t, keys, tr=256):
    t = q_t.shape[1]
    tr = min(tr, t)
    per_key = pl.BlockSpec((PEER_HEADS, N_KEYS, tr), lambda i: (0, 0, i))
    per_key_shape = jax.ShapeDtypeStruct((PEER_HEADS, N_KEYS, t), F32)
    return pl.pallas_call(
        _route_body,
        grid=(t // tr,),
        in_specs=[pl.BlockSpec((PEER_HEADS * D_QUERY, tr), lambda i: (0, i)),
                  pl.BlockSpec((PEER_HEADS * 2, N_KEYS, D_HALF), lambda i: (0, 0, 0))],
        out_specs=[per_key, per_key, per_key, per_key, pl.BlockSpec((PEER_HEADS, tr), lambda i: (0, i))],
        out_shape=[per_key_shape] * 4 + [jax.ShapeDtypeStruct((PEER_HEADS, t), F32)],
        scratch_shapes=[pltpu.VMEM((2, PEER_TOPK, tr), F32),
                        pltpu.VMEM((PEER_TOPK * PEER_TOPK, tr), F32),
                        pltpu.VMEM((PEER_TOPK, tr), F32)],
        compiler_params=_params("parallel"),
        name="peer_route",
    )(q_t, keys)


def _peer_body(h_ref, u_ref, vt_ref, s1_ref, s2_ref, e1_ref, e2_ref, tau_ref, o_ref, acc_ref, w_ref, *, te):
    eb = pl.program_id(1)

    @pl.when(eb == 0)
    def _():
        acc_ref[...] = jnp.zeros_like(acc_ref)

    a_t = lax.dot_general(u_ref[...], h_ref[...], _NT, preferred_element_type=F32)
    rows_per_blk = te // N_KEYS
    for r in range(rows_per_blk):
        i = eb * rows_per_blk + r
        gate = None
        for hd in range(PEER_HEADS):
            s = s1_ref[hd, pl.ds(i, 1), :] + s2_ref[hd]
            g = jnp.where(s >= tau_ref[hd:hd + 1, :], e1_ref[hd, pl.ds(i, 1), :] * e2_ref[hd], 0.0)
            gate = g if gate is None else gate + g
        w_ref[r * N_KEYS:(r + 1) * N_KEYS, :] = (gate * _gelu(a_t[r * N_KEYS:(r + 1) * N_KEYS, :])).astype(BF16)
    acc_ref[...] += jnp.dot(vt_ref[...], w_ref[...], preferred_element_type=F32)

    @pl.when(eb == pl.num_programs(1) - 1)
    def _():
        o_ref[...] = acc_ref[...].T.astype(o_ref.dtype)


def _peer_mix(h, u_tab, v_tab_t, s1, s2, e1, e2, tau, tm=512, te=512):
    t = h.shape[0]
    tm = min(tm, t)
    once = dict(pipeline_mode=pl.Buffered(1))
    per_key = pl.BlockSpec((PEER_HEADS, N_KEYS, tm), lambda i, e: (0, 0, i), **once)
    return pl.pallas_call(
        functools.partial(_peer_body, te=te),
        grid=(t // tm, N_EXPERTS // te),
        in_specs=[
            pl.BlockSpec((tm, D_MODEL), lambda i, e: (i, 0), **once),
            pl.BlockSpec((te, D_MODEL), lambda i, e: (e, 0)),
            pl.BlockSpec((D_MODEL, te), lambda i, e: (0, e)),
            per_key, per_key, per_key, per_key,
            pl.BlockSpec((PEER_HEADS, tm), lambda i, e: (0, i), **once),
        ],
        out_specs=pl.BlockSpec((tm, D_MODEL), lambda i, e: (i, 0)),
        out_shape=jax.ShapeDtypeStruct((t, D_MODEL), BF16),
        scratch_shapes=[pltpu.VMEM((D_MODEL, tm), F32), pltpu.VMEM((te, tm), BF16)],
        compiler_params=_params("parallel", "arbitrary"),
        name="peer_mix",
    )(h, u_tab, v_tab_t, s1, s2, e1, e2, tau)


def _prepare_weights(norm_mix_g, w_in, lambda_q1, lambda_k1, lambda_q2, lambda_k2, subln_g, sgu_ln_g, sgu_ln_b,
                     w_spatial, b_spatial, w_branch_a, w_branch_b, w_out, norm_cross_g, norm_mem_g, w_cq, w_ck,
                     w_cv, w_co, norm_ffn_g, w_peer_q, peer_keys, expert_u, expert_v, final_norm_g):
    l = 0
    return dict(
        norm_mix_g=norm_mix_g[l], w_in=w_in[l].astype(BF16),
        lam_vecs=tuple(v[l].reshape(1, A_HEAD_DIM).astype(F32) for v in (lambda_q1, lambda_k1, lambda_q2, lambda_k2)),
        subln_g=subln_g[l].reshape(1, A_V_DIM).astype(F32),
        sgu_ln_g=sgu_ln_g[l].reshape(1, B_WIDTH).astype(F32), sgu_ln_b=sgu_ln_b[l].reshape(1, B_WIDTH).astype(F32),
        w_spatial=w_spatial[l].astype(BF16), b_spatial=b_spatial[l].reshape(B_GROUPS, CHUNK, 1).astype(F32),
        w_branch_a=w_branch_a[l].astype(BF16), w_branch_b=w_branch_b[l].astype(BF16), w_out=w_out[l].astype(BF16),
        norm_cross_g=norm_cross_g[l], norm_mem_g=norm_mem_g[l],
        w_cq=w_cq[l].astype(BF16), w_ck=w_ck[l].astype(BF16), w_cv=w_cv[l].astype(BF16), w_co=w_co[l].astype(BF16),
        norm_ffn_g=norm_ffn_g[l], w_peer_q_t=w_peer_q[l].T.astype(BF16),
        peer_keys=peer_keys[l].reshape(PEER_HEADS * 2, N_KEYS, D_HALF).astype(BF16),
        expert_u=expert_u[l].astype(BF16), expert_v_t=expert_v[l].T.astype(BF16),
        final_norm_g=final_norm_g,
    )


def _encoder(x, mem, w):
    batch, seq, d = x.shape
    x0 = x.reshape(batch * seq, d)
    xn = _rmsnorm(x0, w["norm_mix_g"], BF16)
    z = _in_proj(xn, w["w_in"], _rope_tables(seq), seq)
    o_a = _diff_attention(z, w["lam_vecs"], w["subln_g"], batch, seq)
    sgu = _sgu(z, w["sgu_ln_g"], w["sgu_ln_b"], w["w_spatial"], w["b_spatial"])
    merged = _branch_merge(o_a, sgu, w["w_branch_a"], w["w_branch_b"], z)
    x1 = _matmul(merged, w["w_out"], residual=x0)
    hc = _rmsnorm(x1, w["norm_cross_g"], BF16)
    mn = _rmsnorm(mem.reshape(batch * N_MEM, d), w["norm_mem_g"], BF16, tm=N_MEM)
    kc = _matmul(mn, w["w_ck"], tm=N_MEM)
    vc = _matmul(mn, w["w_cv"], tm=N_MEM)
    oc = _cross_attention(hc, w["w_cq"], kc, vc, batch, seq)
    x2 = _matmul(oc, w["w_co"], residual=x1)
    hp = _rmsnorm(x2, w["norm_ffn_g"], BF16)
    q_t = _matmul_nt(w["w_peer_q_t"], hp)
    s1, s2, e1, e2, tau = _peer_route(q_t, w["peer_keys"])
    peer = _peer_mix(hp, w["expert_u"], w["expert_v_t"], s1, s2, e1, e2, tau)
    y = _add_rmsnorm(x2, peer, w["final_norm_g"])
    return y.reshape(batch, seq, d)


def kernel(x_prompt, x_sample, mem_prompt, mem_sample, norm_mix_g, w_in, lambda_q1, lambda_k1, lambda_q2,
           lambda_k2, subln_g, sgu_ln_g, sgu_ln_b, w_spatial, b_spatial, w_branch_a, w_branch_b, w_out,
           norm_cross_g, norm_mem_g, w_cq, w_ck, w_cv, w_co, norm_ffn_g, w_peer_q, peer_keys, expert_u, expert_v,
           final_norm_g):
    w = _prepare_weights(norm_mix_g, w_in, lambda_q1, lambda_k1, lambda_q2, lambda_k2, subln_g, sgu_ln_g,
                         sgu_ln_b, w_spatial, b_spatial, w_branch_a, w_branch_b, w_out, norm_cross_g, norm_mem_g,
                         w_cq, w_ck, w_cv, w_co, norm_ffn_g, w_peer_q, peer_keys, expert_u, expert_v,
                         final_norm_g)
    return (_encoder(x_prompt, mem_prompt, w), _encoder(x_sample, mem_sample, w))
```
